```python
import math
import jax
import jax.numpy as jnp
from jax import lax
import numpy as np

D_MODEL = 1024
BATCH = 4
SEQ = 4096
DEPTH = 4
DEC_BATCH = 128
DEC_SEQ = 8
PAST_LEN = 2048
PAGE_SIZE = 128

N_MIXERS = 4
CONV_A_W = 3
S5_GROUP = 16
S5_GROUPS = D_MODEL // S5_GROUP
S5_STATE = 64
S5_CHUNK = 128
DT_MIN = 0.001
DT_MAX = 0.1
N_HEADS = 16
HEAD_DIM = D_MODEL // N_HEADS
DILATED_BRANCHES = ((128, 1), (512, 4), (2048, 16))
MAX_WINDOW = 2048
Q_BLOCK = 128
REL_BUCKETS = 32
REL_MAX_DIST = MAX_WINDOW
CONV_D_W = 31
D_FF = 3584
N_EXPERTS = 8
TOP_K = 2
MOE_BLOCK = 128
N_DENSE = (DEPTH + 1) // 2
N_MOE = DEPTH // 2
PLE_DIM = 256
DN_ALPHA = (2.0 * DEPTH) ** 0.25
DN_BETA = (8.0 * DEPTH) ** -0.25
LN_EPS = 1e-5

kernel_name = 'hybrid_conv_s5_dilated_conformer_decoder_step'


def layer_norm(x, g, b):
    xf = x.astype(jnp.float32)
    mu = jnp.mean(xf, axis=-1, keepdims=True)
    var = jnp.mean(jnp.square(xf - mu), axis=-1, keepdims=True)
    y = (xf - mu) * lax.rsqrt(var + LN_EPS) * g.astype(jnp.float32) + b.astype(jnp.float32)
    return y.astype(x.dtype)


def causal_dwconv(u, buf, w):
    width = w.shape[0]
    z = jnp.concatenate([buf.astype(u.dtype), u], axis=1)
    y = lax.conv_general_dilated(z, w[:, None, :].astype(u.dtype), window_strides=(1,), padding='VALID',
                                 dimension_numbers=('NWC', 'WIO', 'NWC'), feature_group_count=u.shape[-1])
    return y, z[:, z.shape[1] - (width - 1):]


def short_conv_mixer(x, buf, w_in, w_conv, w_out):
    gate_b, gate_c, h = jnp.split(x @ w_in, 3, axis=-1)
    y, new_buf = causal_dwconv(gate_c * h, buf, w_conv)
    return (gate_b * y) @ w_out, new_buf


def cmul(ar, ai, br, bi):
    return ar * br - ai * bi, ar * bi + ai * br


def s5_discretize(lam_re, lam_im, log_dt, b_re, b_im):
    lr = lam_re.astype(jnp.float32)
    li = lam_im.astype(jnp.float32)
    dt = jnp.exp(log_dt.astype(jnp.float32))[:, None]
    mag = jnp.exp(dt * lr)
    ar, ai = mag * jnp.cos(dt * li), mag * jnp.sin(dt * li)
    inv_den = 1.0 / (lr * lr + li * li)
    zr = ((ar - 1.0) * lr + ai * li) * inv_den
    zi = (ai * lr - (ar - 1.0) * li) * inv_den
    bbr, bbi = cmul(zr[..., None], zi[..., None], b_re.astype(jnp.float32), b_im.astype(jnp.float32))
    return ar, ai, bbr, bbi


def s5_combine(e1, e2):
    a1r, a1i, b1r, b1i = e1
    a2r, a2i, b2r, b2i = e2
    ar, ai = cmul(a1r, a1i, a2r, a2i)
    br, bi = cmul(a2r, a2i, b1r, b1i)
    return ar, ai, br + b2r, bi + b2i


def s5_chunk(h_re, h_im, u, ar, ai, bbr, bbi, c_re, c_im, d_skip):
    bu_r = jnp.einsum('btgp,gnp->btgn', u, bbr)
    bu_i = jnp.einsum('btgp,gnp->btgn', u, bbi)
    a_r = jnp.broadcast_to(ar, bu_r.shape)
    a_i = jnp.broadcast_to(ai, bu_i.shape)
    pr, pi, sr, si = lax.associative_scan(s5_combine, (a_r, a_i, bu_r, bu_i), axis=1)
    hr, hi = cmul(pr, pi, h_re[:, None], h_im[:, None])
    xr, xi = hr + sr, hi + si
    y = (jnp.einsum('btgn,gpn->btgp', xr, c_re) - jnp.einsum('btgn,gpn->btgp', xi, c_im)
         + d_skip * u)
    return y, xr[:, -1], xi[:, -1]


def s5_mixer(x, h_re, h_im, lam_re, lam_im, log_dt, b_re, b_im, c_re, c_im, d_skip, w_glu):
    b, t, _ = x.shape
    ar, ai, bbr, bbi = s5_discretize(lam_re, lam_im, log_dt, b_re, b_im)
    cr, ci, dd = c_re.astype(jnp.float32), c_im.astype(jnp.float32), d_skip.astype(jnp.float32)
    u = x.astype(jnp.float32).reshape(b, t, S5_GROUPS, S5_GROUP)
    chunk = S5_CHUNK if t % S5_CHUNK == 0 else t
    uc = u.reshape(b, t // chunk, chunk, S5_GROUPS, S5_GROUP).swapaxes(0, 1)

    def step(carry, u_blk):
        y_blk, hr, hi = s5_chunk(carry[0], carry[1], u_blk, ar, ai, bbr, bbi, cr, ci, dd)
        return (hr, hi), y_blk

    (hr, hi), ys = lax.scan(step, (h_re.astype(jnp.float32), h_im.astype(jnp.float32)), uc)
    y = jax.nn.gelu(ys.swapaxes(0, 1).reshape(b, t, D_MODEL)).astype(x.dtype)
    val, gate = jnp.split(y @ w_glu, 2, axis=-1)
    return val * jax.nn.sigmoid(gate), hr.astype(h_re.dtype), hi.astype(h_im.dtype)


def t5_bucket(dist):
    exact = REL_BUCKETS // 2
    d = np.maximum(dist, 1).astype(np.float32)
    large = exact + (np.log(d / exact) / np.log(REL_MAX_DIST / exact) * (REL_BUCKETS - exact)).astype(np.int32)
    return np.where(dist < exact, dist, np.minimum(large, REL_BUCKETS - 1)).astype(np.int32)


def dilated_block(q, k_all, v_all, q_idx, rel_bias):
    maxes, sums, outs = [], [], []
    for window, dil in DILATED_BRANCHES:
        offs = np.arange(window // dil + 1, dtype=np.int32) * dil
        bias = rel_bias[t5_bucket(offs)].astype(jnp.float32).T
        idx = q_idx[:, None] - jnp.asarray(offs)[None, :]
        valid = idx >= 0
        idx = jnp.maximum(idx, 0)
        k_g = jnp.take(k_all, idx, axis=1)
        v_g = jnp.take(v_all, idx, axis=1)
        logits = jnp.einsum('bqhd,bqjhd->bhqj', q, k_g, preferred_element_type=jnp.float32) + bias[None, :, None, :]
        logits = jnp.where(valid[None, None], logits, -jnp.inf)
        m = jnp.max(logits, axis=-1)
        e = jnp.exp(logits - m[..., None])
        maxes.append(m)
        sums.append(jnp.sum(e, axis=-1))
        outs.append(jnp.einsum('bhqj,bqjhd->bqhd', e, v_g.astype(jnp.float32)))
    m_all = jnp.stack(maxes)
    scale = jnp.exp(m_all - jnp.max(m_all, axis=0))
    den = jnp.sum(scale * jnp.stack(sums), axis=0)
    num = jnp.sum(jnp.swapaxes(scale, 2, 3)[..., None] * jnp.stack(outs), axis=0)
    return num / jnp.swapaxes(den, 1, 2)[..., None]


def dilated_attention_mixer(x, k_cache, v_cache, w_qkv, w_o, rel_bias):
    b, t, _ = x.shape
    q, k, v = jnp.split(x @ w_qkv, 3, axis=-1)
    q = (q * HEAD_DIM ** -0.5).reshape(b, t, N_HEADS, HEAD_DIM)
    k = k.reshape(b, t, N_HEADS, HEAD_DIM)
    v = v.reshape(b, t, N_HEADS, HEAD_DIM)
    if k_cache is None:
        k_all, v_all, offset = k, v, 0
        keep = min(MAX_WINDOW, t)
        k_keep, v_keep = k[:, t - keep:], v[:, t - keep:]
    else:
        k_all = jnp.concatenate([k_cache.astype(k.dtype), k], axis=1)
        v_all = jnp.concatenate([v_cache.astype(v.dtype), v], axis=1)
        offset = k_cache.shape[1]
        k_keep, v_keep = k, v
    if t % Q_BLOCK == 0:
        nb = t // Q_BLOCK
        qb = q.reshape(b, nb, Q_BLOCK, N_HEADS, HEAD_DIM).swapaxes(0, 1)
        starts = offset + Q_BLOCK * jnp.arange(nb, dtype=jnp.int32)
        ob = lax.map(lambda a: dilated_block(a[0], k_all, v_all, a[1] + jnp.arange(Q_BLOCK, dtype=jnp.int32), rel_bias),
                     (qb, starts))
        o = ob.swapaxes(0, 1).reshape(b, t, D_MODEL)
    else:
        o = dilated_block(q, k_all, v_all, offset + jnp.arange(t, dtype=jnp.int32), rel_bias).reshape(b, t, D_MODEL)
    return o.astype(x.dtype) @ w_o, k_keep, v_keep


def conformer_conv_mixer(x, buf, w_pw1, w_dw, ln_g, ln_b, w_pw2):
    val, gate = jnp.split(x @ w_pw1, 2, axis=-1)
    y, new_buf = causal_dwconv(val * jax.nn.sigmoid(gate), buf, w_dw)
    y = jax.nn.silu(layer_norm(y, ln_g, ln_b))
    return y @ w_pw2, new_buf


def swiglu(x, w_in, w_out):
    a, b = jnp.split(x @ w_in, 2, axis=-1)
    return (jax.nn.silu(a) * b) @ w_out


def moe_swiglu(x, w_router, w_in, w_out):
    b, t, d = x.shape
    xt = x.reshape(-1, d)
    n = xt.shape[0]
    logits = jnp.dot(xt, w_router, preferred_element_type=jnp.float32)
    top_v, top_e = lax.top_k(logits, TOP_K)
    gates = jax.nn.softmax(top_v, axis=-1)
    flat_e = top_e.reshape(-1)
    flat_tok = jnp.repeat(jnp.arange(n, dtype=jnp.int32), TOP_K)
    flat_g = gates.reshape(-1)
    order = jnp.argsort(flat_e)
    se, stok, sg = flat_e[order], flat_tok[order], flat_g[order]
    counts = jnp.zeros((N_EXPERTS,), jnp.int32).at[flat_e].add(1)
    padded = (counts + MOE_BLOCK - 1) // MOE_BLOCK * MOE_BLOCK
    pad_end = jnp.cumsum(padded)
    pad_start = pad_end - padded
    start = jnp.cumsum(counts) - counts
    dest = pad_start[se] + jnp.arange(n * TOP_K, dtype=jnp.int32) - start[se]
    n_blocks = -(-(n * TOP_K) // MOE_BLOCK) + N_EXPERTS
    cap = n_blocks * MOE_BLOCK
    slot_tok = jnp.full((cap,), n, jnp.int32).at[dest].set(stok)
    slot_g = jnp.zeros((cap,), jnp.float32).at[dest].set(sg)
    block_e = jnp.minimum(jnp.searchsorted(pad_end, MOE_BLOCK * jnp.arange(n_blocks, dtype=jnp.int32), side='right'),
                          N_EXPERTS - 1)
    x_pad = jnp.concatenate([xt, jnp.zeros((1, d), xt.dtype)], axis=0)
    xb = x_pad[slot_tok].reshape(n_blocks, MOE_BLOCK, d)

    def expert_block(args):
        x_blk, e = args
        return swiglu(x_blk, w_in[e], w_out[e])

    yb = lax.map(expert_block, (xb, block_e)).reshape(cap, d)
    y = jnp.zeros((n + 1, d), jnp.float32).at[slot_tok].add(yb.astype(jnp.float32) * slot_g[:, None])[:n]
    return y.astype(x.dtype).reshape(b, t, d)


def trunk(x, p, conv_a_buf, s5_re, s5_im, k_cache, v_cache, conv_d_buf, w):
    k_new = v_new = None
    for i in range(DEPTH):
        kind = i % N_MIXERS
        if kind == 0:
            h, conv_a_buf = short_conv_mixer(x, conv_a_buf, w['a_w_in'], w['a_conv'], w['a_w_out'])
        elif kind == 1:
            h, s5_re, s5_im = s5_mixer(x, s5_re, s5_im, w['s5_lam_re'], w['s5_lam_im'], w['s5_log_dt'],
                                       w['s5_b_re'], w['s5_b_im'], w['s5_c_re'], w['s5_c_im'], w['s5_d'], w['s5_w_glu'])
        elif kind == 2:
            h, k_new, v_new = dilated_attention_mixer(x, k_cache, v_cache, w['c_w_qkv'], w['c_w_o'], w['rel_bias'])
        else:
            h, conv_d_buf = conformer_conv_mixer(x, conv_d_buf, w['d_w_pw1'], w['d_dw'], w['d_ln_g'], w['d_ln_b'],
                                                 w['d_w_pw2'])
        x = layer_norm(DN_ALPHA * x + h, w['ln_mix_g'][i], w['ln_mix_b'][i])
        if i % 2 == 0:
            f = swiglu(x, w['ffn_w_in'][i // 2], w['ffn_w_out'][i // 2])
        else:
            f = moe_swiglu(x, w['moe_router'][i // 2], w['moe_w_in'][i // 2], w['moe_w_out'][i // 2])
        x = layer_norm(DN_ALPHA * x + f, w['ln_ffn_g'][i], w['ln_ffn_b'][i])
        x = x + jax.nn.sigmoid(x @ w['ple_w_gate'][i]) * (p[i] @ w['ple_w_proj'][i])
    return x, conv_a_buf, s5_re, s5_im, k_new, v_new, conv_d_buf


def setup_inputs(seed: int = 0) -> dict:
    key = jax.random.key(seed)
    keys = iter(jax.random.split(key, 64))

    def nrm(shape, scale):
        return scale * jax.random.normal(next(keys), shape, jnp.float32)

    d, g, n, pg, f, e = D_MODEL, S5_GROUPS, S5_STATE, S5_GROUP, D_FF, N_EXPERTS
    win = min(MAX_WINDOW, PAST_LEN)
    inv_d = d ** -0.5
    w_qkv = nrm((d, 3 * d), inv_d)
    w_qkv = w_qkv.at[:, 2 * d:].multiply(DN_BETA)
    lam_im = math.pi * jnp.arange(n, dtype=jnp.float32)[None, :] + nrm((g, n), 0.01)
    log_dt = jax.random.uniform(next(keys), (g,), jnp.float32, math.log(DT_MIN), math.log(DT_MAX))
    return {
        'x_prompt': nrm((BATCH, SEQ, d), 1.0),
        'x_sample': nrm((DEC_BATCH, DEC_SEQ, d), 1.0),
        'state_conv_a': nrm((DEC_BATCH, CONV_A_W - 1, d), 1.0),
        'state_s5_re': nrm((DEC_BATCH, g, n), 0.1),
        'state_s5_im': nrm((DEC_BATCH, g, n), 0.1),
        'cache_k_win': nrm((DEC_BATCH, win, N_HEADS, HEAD_DIM), 1.0),
        'cache_v_win': nrm((DEC_BATCH, win, N_HEADS, HEAD_DIM), 1.0),
        'state_conv_d': nrm((DEC_BATCH, CONV_D_W - 1, d), 0.5),
        'p_prompt': nrm((DEPTH, BATCH, SEQ, PLE_DIM), 1.0),
        'p_sample': nrm((DEPTH, DEC_BATCH, DEC_SEQ, PLE_DIM), 1.0),
        'a_w_in': nrm((d, 3 * d), inv_d),
        'a_conv': nrm((CONV_A_W, d), CONV_A_W ** -0.5),
        'a_w_out': nrm((d, d), inv_d * DN_BETA),
        's5_lam_re': -0.5 + nrm((g, n), 0.01),
        's5_lam_im': lam_im,
        's5_log_dt': log_dt,
        's5_b_re': nrm((g, n, pg), (2.0 * pg) ** -0.5),
        's5_b_im': nrm((g, n, pg), (2.0 * pg) ** -0.5),
        's5_c_re': nrm((g, pg, n), n ** -0.5),
        's5_c_im': nrm((g, pg, n), n ** -0.5),
        's5_d': nrm((g, pg), 1.0),
        's5_w_glu': nrm((d, 2 * d), inv_d * DN_BETA),
        'c_w_qkv': w_qkv,
        'c_w_o': nrm((d, d), inv_d * DN_BETA),
        'rel_bias': nrm((REL_BUCKETS, N_HEADS), 0.5),
        'd_w_pw1': nrm((d, 2 * d), inv_d),
        'd_dw': nrm((CONV_D_W, d), CONV_D_W ** -0.5),
        'd_ln_g': 1.0 + nrm((d,), 0.02),
        'd_ln_b': nrm((d,), 0.02),
        'd_w_pw2': nrm((d, d), inv_d * DN_BETA),
        'ffn_w_in': nrm((N_DENSE, d, 2 * f), inv_d),
        'ffn_w_out': nrm((N_DENSE, f, d), f ** -0.5 * DN_BETA),
        'moe_router': nrm((N_MOE, d, e), inv_d),
        'moe_w_in': nrm((N_MOE, e, d, 2 * f), inv_d),
        'moe_w_out': nrm((N_MOE, e, f, d), f ** -0.5 * DN_BETA),
        'ln_mix_g': 1.0 + nrm((DEPTH, d), 0.02),
        'ln_mix_b': nrm((DEPTH, d), 0.02),
        'ln_ffn_g': 1.0 + nrm((DEPTH, d), 0.02),
        'ln_ffn_b': nrm((DEPTH, d), 0.02),
        'ple_w_proj': nrm((DEPTH, PLE_DIM, d), PLE_DIM ** -0.5),
        'ple_w_gate': nrm((DEPTH, d, d), inv_d),
    }


def reference(x_prompt, x_sample, state_conv_a, state_s5_re, state_s5_im, cache_k_win, cache_v_win, state_conv_d,
              p_prompt, p_sample, a_w_in, a_conv, a_w_out, s5_lam_re, s5_lam_im, s5_log_dt, s5_b_re, s5_b_im,
              s5_c_re, s5_c_im, s5_d, s5_w_glu, c_w_qkv, c_w_o, rel_bias, d_w_pw1, d_dw, d_ln_g, d_ln_b, d_w_pw2,
              ffn_w_in, ffn_w_out, moe_router, moe_w_in, moe_w_out, ln_mix_g, ln_mix_b, ln_ffn_g, ln_ffn_b,
              ple_w_proj, ple_w_gate):
    w = dict(a_w_in=a_w_in, a_conv=a_conv, a_w_out=a_w_out, s5_lam_re=s5_lam_re, s5_lam_im=s5_lam_im,
             s5_log_dt=s5_log_dt, s5_b_re=s5_b_re, s5_b_im=s5_b_im, s5_c_re=s5_c_re, s5_c_im=s5_c_im, s5_d=s5_d,
             s5_w_glu=s5_w_glu, c_w_qkv=c_w_qkv, c_w_o=c_w_o, rel_bias=rel_bias, d_w_pw1=d_w_pw1, d_dw=d_dw,
             d_ln_g=d_ln_g, d_ln_b=d_ln_b, d_w_pw2=d_w_pw2, ffn_w_in=ffn_w_in, ffn_w_out=ffn_w_out,
             moe_router=moe_router, moe_w_in=moe_w_in, moe_w_out=moe_w_out, ln_mix_g=ln_mix_g, ln_mix_b=ln_mix_b,
             ln_ffn_g=ln_ffn_g, ln_ffn_b=ln_ffn_b, ple_w_proj=ple_w_proj, ple_w_gate=ple_w_gate)
    bp = x_prompt.shape[0]
    zero_a = jnp.zeros((bp, CONV_A_W - 1, D_MODEL), x_prompt.dtype)
    zero_s5 = jnp.zeros((bp, S5_GROUPS, S5_STATE), state_s5_re.dtype)
    zero_d = jnp.zeros((bp, CONV_D_W - 1, D_MODEL), x_prompt.dtype)
    y_prompt, ca_p, s5r_p, s5i_p, k_p, v_p, cd_p = trunk(x_prompt, p_prompt, zero_a, zero_s5, zero_s5, None, None,
                                                         zero_d, w)
    y_sample, ca_s, s5r_s, s5i_s, k_s, v_s, cd_s = trunk(x_sample, p_sample, state_conv_a, state_s5_re, state_s5_im,
                                                         cache_k_win, cache_v_win, state_conv_d, w)
    return (y_prompt, y_sample, ca_p, ca_s, s5r_p, s5i_p, s5r_s, s5i_s, k_p, v_p, k_s, v_s, cd_p, cd_s)
```

```python
import functools
import math

import numpy as np
import jax
import jax.numpy as jnp
from jax import lax
from jax.experimental import pallas as pl
from jax.experimental.pallas import tpu as pltpu

D_MODEL = 1024
DEPTH = 4
CONV_A_W = 3
CONV_D_W = 31
S5_GROUP = 16
S5_GROUPS = D_MODEL // S5_GROUP
S5_STATE = 64
S5_CH = S5_GROUPS * S5_STATE
N_HEADS = 16
HEAD_DIM = D_MODEL // N_HEADS
DILATED_BRANCHES = ((128, 1), (512, 4), (2048, 16))
ATT_BLOCK = 128
REL_BUCKETS = 32
REL_MAX_DIST = 2048
D_FF = 3584
N_EXPERTS = 8
PLE_DIM = 256
DN_ALPHA = (2.0 * DEPTH) ** 0.25
LN_EPS = 1e-5
NEG_BIG = -1e30

V7X_VMEM_BYTES = 64 * 1024 * 1024
V7X_SUBLANES = 8
V7X_LANES = 128

BF16 = jnp.bfloat16
F32 = jnp.float32


def _cparams(sem, vmem_mb):
    return pltpu.CompilerParams(dimension_semantics=sem, vmem_limit_bytes=int(vmem_mb * 1024 * 1024))


def _layer_norm(v, g, b):
    mu = jnp.mean(v, axis=-1, keepdims=True)
    c = v - mu
    var = jnp.mean(c * c, axis=-1, keepdims=True)
    return c * lax.rsqrt(var + LN_EPS) * g + b


def _sigmoid(v):
    return 1.0 / (1.0 + jnp.exp(-v))


def _bdot(a, b):
    return jnp.dot(a.astype(BF16), b.astype(BF16), preferred_element_type=F32)


def _conv_mixer_kernel(*refs, mode, R, W, TB, HALO, nblk, shifted):
    if mode == 'a':
        (x_ref, halo_ref, w1_ref, wc_ref, w2_ref, g_ref, b_ref, o_ref, st_ref, zbuf, zsh, ybuf) = refs
        dg_ref = db_ref = None
    else:
        (x_ref, halo_ref, w1_ref, wc_ref, w2_ref, g_ref, b_ref, dg_ref, db_ref, o_ref, st_ref, zbuf, zsh,
         ybuf) = refs
    i = pl.program_id(1)

    @pl.when(i == 0)
    def _():
        zbuf[0:HALO, :] = halo_ref[...].reshape(HALO, D_MODEL)

    xb = x_ref[...].reshape(TB, D_MODEL)
    t = _bdot(xb, w1_ref[...])
    if mode == 'a':
        gate_b = t[:, :D_MODEL]
        u = t[:, D_MODEL:2 * D_MODEL] * t[:, 2 * D_MODEL:]
    else:
        gate_b = None
        u = t[:, :D_MODEL] * _sigmoid(t[:, D_MODEL:])
    zbuf[HALO:HALO + TB, :] = u
    if shifted:
        zsh[...] = pltpu.roll(zbuf[...], 4, 0)

    CH = 32

    def chunk(c, carry):
        r0 = pl.multiple_of(c * CH, CH)
        acc = jnp.zeros((CH, D_MODEL), F32)
        for k in range(W):
            off = HALO - (W - 1 - k) * R
            if off % V7X_SUBLANES == 0:
                z = zbuf[pl.ds(r0 + off, CH), :]
            else:
                z = zsh[pl.ds(r0 + off + 4, CH), :]
            acc = acc + wc_ref[k:k + 1, :] * z
        ybuf[pl.ds(r0, CH), :] = acc
        return carry

    lax.fori_loop(0, TB // CH, chunk, 0)
    y = ybuf[...]
    if mode == 'a':
        h_in = gate_b * y
    else:
        yn = _layer_norm(y, dg_ref[...], db_ref[...])
        h_in = yn * _sigmoid(yn)
    h = _bdot(h_in, w2_ref[...])
    o_ref[...] = _layer_norm(DN_ALPHA * xb + h, g_ref[...], b_ref[...]).reshape(o_ref.shape)
    last = zbuf[TB:TB + HALO, :]
    st_ref[...] = last.reshape(st_ref.shape)
    if nblk > 1:
        zbuf[0:HALO, :] = last


def _conv_mixer(x, halo, w1, wc, w2, g, b, dln, *, mode, R, TB):
    W = wc.shape[0]
    HALO = (W - 1) * R
    const = lambda c, i: (0, 0)
    if x.ndim == 2:
        nblk, nchunk = x.shape[0] // TB, 1
        assert x.shape[0] % TB == 0
        x_spec = pl.BlockSpec((TB, D_MODEL), lambda c, i: (i, 0))
        h_spec = pl.BlockSpec((HALO, D_MODEL), const)
    else:
        nblk, nchunk = 1, x.shape[1] // R
        assert x.shape[0] * R == TB and R % V7X_SUBLANES == 0
        x_spec = pl.BlockSpec((x.shape[0], R, D_MODEL), lambda c, i: (0, c, 0))
        h_spec = pl.BlockSpec((W - 1, R, D_MODEL), lambda c, i: (0, c, 0))
    assert HALO % V7X_SUBLANES == 0 and (nblk == 1 or TB >= HALO)
    shifted = (R % V7X_SUBLANES) != 0
    assert not shifted or R == 4
    in_specs = [x_spec,
                h_spec,
                pl.BlockSpec(w1.shape, const),
                pl.BlockSpec(wc.shape, const),
                pl.BlockSpec(w2.shape, const),
                pl.BlockSpec((1, D_MODEL), const),
                pl.BlockSpec((1, D_MODEL), const)]
    args = [x, halo, w1, wc, w2, g, b]
    if mode == 'd':
        in_specs += [pl.BlockSpec((1, D_MODEL), const), pl.BlockSpec((1, D_MODEL), const)]
        args += [dln[0], dln[1]]
    zrows = HALO + TB
    return pl.pallas_call(
        functools.partial(_conv_mixer_kernel, mode=mode, R=R, W=W, TB=TB, HALO=HALO, nblk=nblk, shifted=shifted),
        grid=(nchunk, nblk),
        in_specs=in_specs,
        out_specs=[x_spec, h_spec],
        out_shape=[jax.ShapeDtypeStruct(x.shape, F32), jax.ShapeDtypeStruct(halo.shape, F32)],
        scratch_shapes=[pltpu.VMEM((zrows, D_MODEL), F32),
                        pltpu.VMEM((zrows if shifted else V7X_SUBLANES, D_MODEL), F32),
                        pltpu.VMEM((TB, D_MODEL), F32)],
        compiler_params=_cparams(("arbitrary", "arbitrary"), 56),
        name=f"conv_mixer_{mode}_r{R}",
    )(*args)


def _s5_disc_kernel(lr_ref, li_ref, ldt_ref, bre_ref, bim_ref, ar_ref, ai_ref, bbr_ref, bbi_ref):
    lr = lr_ref[...]
    li = li_ref[...]
    dt = jnp.exp(ldt_ref[...])
    mag = jnp.exp(dt * lr)
    ar = mag * jnp.cos(dt * li)
    ai = mag * jnp.sin(dt * li)
    inv_den = 1.0 / (lr * lr + li * li)
    zr = ((ar - 1.0) * lr + ai * li) * inv_den
    zi = (ai * lr - (ar - 1.0) * li) * inv_den
    ar_ref[...] = ar
    ai_ref[...] = ai
    bre = bre_ref[...]
    bim = bim_ref[...]
    bbr_ref[...] = zr[None] * bre - zi[None] * bim
    bbi_ref[...] = zr[None] * bim + zi[None] * bre


def _s5_discretize(lam_re, lam_im, log_dt, b_re, b_im):
    g, n, p = b_re.shape
    outs = pl.pallas_call(
        _s5_disc_kernel,
        out_shape=[jax.ShapeDtypeStruct((g, n), F32), jax.ShapeDtypeStruct((g, n), F32),
                   jax.ShapeDtypeStruct((p, g, n), F32), jax.ShapeDtypeStruct((p, g, n), F32)],
        name="s5_discretize",
    )(lam_re, lam_im, log_dt.reshape(g, 1), jnp.transpose(b_re, (2, 0, 1)), jnp.transpose(b_im, (2, 0, 1)))
    return outs


S5_JB = 4
S5_GPB = S5_GROUPS // S5_JB


def _s5_block_weights(bbr, bbi, c_re, c_im):
    p, g, n = bbr.shape
    eye = jnp.eye(S5_GPB, dtype=F32)

    def wb(bb):
        r = bb.reshape(p, S5_JB, S5_GPB, n)
        return jnp.einsum('pjgn,gh->jgphn', r, eye).reshape(S5_JB, S5_GPB * p, S5_GPB * n)

    def wc(c):
        r = c.reshape(S5_JB, S5_GPB, p, n)
        return jnp.einsum('jgpn,gh->jgnhp', r, eye).reshape(S5_JB, S5_GPB * n, S5_GPB * p)

    w_b = jnp.concatenate([wb(bbr), wb(bbi)], axis=-1).astype(BF16)
    return w_b, wc(c_re).astype(BF16), wc(c_im).astype(BF16)


def _cmul_add(ar, ai, xr, xi, tr, ti):
    return ar * xr - ai * xi + tr, ar * xi + ai * xr + ti


def _s5_kernel(x_ref, h0_ref, a_ref, wb_ref, wcr_ref, wci_ref, dsk_ref, wglu_ref, g_ref, b_ref,
               o_ref, st_ref, bu_ref, h_ref, *, R, TB):
    tb = pl.program_id(1)
    LW = 512
    NJ = S5_CH // S5_JB
    xb = x_ref[...].reshape(TB, D_MODEL)

    @pl.when(tb == 0)
    def _():
        h_ref[...] = h0_ref[...]

    for j in range(S5_JB):
        r = _bdot(xb[:, j * 256:(j + 1) * 256], wb_ref[j])
        bu_ref[:, j * NJ:(j + 1) * NJ] = r[:, :NJ]
        bu_ref[:, S5_CH + j * NJ:S5_CH + (j + 1) * NJ] = r[:, NJ:]

    if R == 4:
        low = lax.broadcasted_iota(jnp.int32, (V7X_SUBLANES, LW), 0) < 4
        for c in range(S5_CH // LW):
            lr = pl.ds(c * LW, LW)
            li = pl.ds(S5_CH + c * LW, LW)
            ar = a_ref[:, lr]
            ai = a_ref[:, li]

            def tile(j, carry, lr=lr, li=li, ar=ar, ai=ai):
                cr, ci = carry
                r0 = pl.multiple_of(j * V7X_SUBLANES, V7X_SUBLANES)
                tr = bu_ref[pl.ds(r0, V7X_SUBLANES), lr]
                ti = bu_ref[pl.ds(r0, V7X_SUBLANES), li]
                er, ei = _cmul_add(ar, ai, pltpu.roll(cr, 4, 0), pltpu.roll(ci, 4, 0), tr, ti)
                odr, odi = _cmul_add(ar, ai, pltpu.roll(er, 4, 0), pltpu.roll(ei, 4, 0), tr, ti)
                bu_ref[pl.ds(r0, V7X_SUBLANES), lr] = jnp.where(low, er, odr)
                bu_ref[pl.ds(r0, V7X_SUBLANES), li] = jnp.where(low, ei, odi)
                return odr, odi

            cr, ci = lax.fori_loop(0, TB // V7X_SUBLANES, tile, (h_ref[:, lr], h_ref[:, li]), unroll=2)
            h_ref[:, lr] = cr
            h_ref[:, li] = ci
    else:
        nsteps = TB // R
        for c in range(S5_CH // LW):
            lr = pl.ds(c * LW, LW)
            li = pl.ds(S5_CH + c * LW, LW)
            ar = a_ref[:, lr]
            ai = a_ref[:, li]

            def seqtile(s, carry, lr=lr, li=li, ar=ar, ai=ai):
                s0 = pl.multiple_of(s * V7X_SUBLANES, V7X_SUBLANES)
                cr = h_ref[pl.ds(s0, V7X_SUBLANES), lr]
                ci = h_ref[pl.ds(s0, V7X_SUBLANES), li]
                for t in range(nsteps):
                    rows = pl.ds(s0 + t * R, V7X_SUBLANES)
                    cr, ci = _cmul_add(ar, ai, cr, ci, bu_ref[rows, lr], bu_ref[rows, li])
                    bu_ref[rows, lr] = cr
                    bu_ref[rows, li] = ci
                h_ref[pl.ds(s0, V7X_SUBLANES), lr] = cr
                h_ref[pl.ds(s0, V7X_SUBLANES), li] = ci
                return carry

            lax.fori_loop(0, R // V7X_SUBLANES, seqtile, 0)

    st_ref[...] = h_ref[...]
    ys = []
    for j in range(S5_JB):
        xr = bu_ref[:, j * NJ:(j + 1) * NJ]
        xi = bu_ref[:, S5_CH + j * NJ:S5_CH + (j + 1) * NJ]
        ys.append(_bdot(xr, wcr_ref[j]) - _bdot(xi, wci_ref[j]))
    y = jnp.concatenate(ys, axis=-1) + dsk_ref[...] * xb
    y = jax.nn.gelu(y, approximate=True)
    vg = _bdot(y, wglu_ref[...])
    h = vg[:, :D_MODEL] * _sigmoid(vg[:, D_MODEL:])
    o_ref[...] = _layer_norm(DN_ALPHA * xb + h, g_ref[...], b_ref[...]).reshape(o_ref.shape)


def _s5_mixer(x, h0, a8, w_b, wc_re, wc_im, dsk, wglu, g, b, *, R, TB, nchunk):
    hrows = max(R, V7X_SUBLANES)
    const2 = lambda c, t: (0, 0)
    const3 = lambda c, t: (0, 0, 0)
    if x.ndim == 2:
        ntime = x.shape[0] // TB
        x_spec = pl.BlockSpec((TB, D_MODEL), lambda c, t: (t, 0))
    else:
        ntime = 1
        assert x.shape[0] * R == TB
        x_spec = pl.BlockSpec((x.shape[0], R, D_MODEL), lambda c, t: (0, c, 0))
    return pl.pallas_call(
        functools.partial(_s5_kernel, R=R, TB=TB),
        grid=(nchunk, ntime),
        in_specs=[x_spec,
                  pl.BlockSpec((hrows, 2 * S5_CH), lambda c, t: (c, 0)),
                  pl.BlockSpec(a8.shape, const2),
                  pl.BlockSpec(w_b.shape, const3),
                  pl.BlockSpec(wc_re.shape, const3),
                  pl.BlockSpec(wc_im.shape, const3),
                  pl.BlockSpec((1, D_MODEL), const2),
                  pl.BlockSpec(wglu.shape, const2),
                  pl.BlockSpec((1, D_MODEL), const2),
                  pl.BlockSpec((1, D_MODEL), const2)],
        out_specs=[x_spec, pl.BlockSpec((hrows, 2 * S5_CH), lambda c, t: (c, 0))],
        out_shape=[jax.ShapeDtypeStruct(x.shape, F32), jax.ShapeDtypeStruct((hrows * nchunk, 2 * S5_CH), F32)],
        scratch_shapes=[pltpu.VMEM((TB, 2 * S5_CH), F32), pltpu.VMEM((hrows, 2 * S5_CH), F32)],
        compiler_params=_cparams(("arbitrary", "arbitrary"), 56),
        name=f"s5_mixer_r{R}",
    )(x, h0, a8, w_b, wc_re, wc_im, dsk, wglu, g, b)


def _qkv_kernel(x_ref, w_ref, o_ref):
    r = _bdot(x_ref[...], w_ref[...])
    o_ref[:, :D_MODEL] = r[:, :D_MODEL] * (HEAD_DIM ** -0.5)
    o_ref[:, D_MODEL:] = r[:, D_MODEL:]


def _qkv_proj(x, w, *, TM):
    n = x.shape[0]
    return pl.pallas_call(
        _qkv_kernel,
        grid=(n // TM,),
        in_specs=[pl.BlockSpec((TM, D_MODEL), lambda i: (i, 0)), pl.BlockSpec(w.shape, lambda i: (0, 0))],
        out_specs=pl.BlockSpec((TM, 3 * D_MODEL), lambda i: (i, 0)),
        out_shape=jax.ShapeDtypeStruct((n, 3 * D_MODEL), F32),
        compiler_params=_cparams(("parallel",), 48),
        name="qkv_proj",
    )(x, w)


def _proj_ln_kernel(a_ref, x_ref, w_ref, g_ref, b_ref, o_ref):
    h = _bdot(a_ref[...], w_ref[...])
    o_ref[...] = _layer_norm(DN_ALPHA * x_ref[...] + h, g_ref[...], b_ref[...])


def _proj_ln(a, x, w, g, b, *, TM):
    n = x.shape[0]
    row = lambda i: (i, 0)
    const = lambda i: (0, 0)
    return pl.pallas_call(
        _proj_ln_kernel,
        grid=(n // TM,),
        in_specs=[pl.BlockSpec((TM, D_MODEL), row), pl.BlockSpec((TM, D_MODEL), row), pl.BlockSpec(w.shape, const),
                  pl.BlockSpec((1, D_MODEL), const), pl.BlockSpec((1, D_MODEL), const)],
        out_specs=pl.BlockSpec((TM, D_MODEL), row),
        out_shape=jax.ShapeDtypeStruct((n, D_MODEL), F32),
        compiler_params=_cparams(("parallel",), 48),
        name="attn_out_proj_ln",
    )(a, x, w, g, b)


def _t5_bucket(dist):
    exact = REL_BUCKETS // 2
    d = np.maximum(dist, 1).astype(np.float32)
    large = exact + (np.log(d / exact) / np.log(REL_MAX_DIST / exact) * (REL_BUCKETS - exact)).astype(np.int32)
    return np.where(dist < exact, dist, np.minimum(large, REL_BUCKETS - 1)).astype(np.int32)


def _prompt_bias_tables(rel_bias):
    qi = np.arange(ATT_BLOCK)[:, None]
    col = np.arange(2 * ATT_BLOCK)[None, :]
    j = qi + ATT_BLOCK - col
    valid = (j >= 0) & (j <= ATT_BLOCK)
    valid_first = valid & (col >= ATT_BLOCK)
    jc = np.clip(j, 0, ATT_BLOCK)
    tabs = []
    for _, dil in DILATED_BRANCHES:
        bucket = _t5_bucket(jc * dil)
        bias = jnp.transpose(rel_bias.astype(F32)[bucket], (2, 0, 1))
        tabs.append(jnp.stack([jnp.where(valid[None], bias, NEG_BIG), jnp.where(valid_first[None], bias, NEG_BIG)]))
    return jnp.stack(tabs)


def _attn_prompt_kernel(q_ref, k_ref, v_ref, bm_ref, o_ref, m_ref, l_ref, *, T):
    B = ATT_BLOCK
    lane = lax.broadcasted_iota(jnp.int32, (B, V7X_LANES), 1)
    head_a = lane < HEAD_DIM
    m_ref[...] = jnp.full(m_ref.shape, NEG_BIG, F32)
    l_ref[...] = jnp.zeros(l_ref.shape, F32)
    o_ref[...] = jnp.zeros(o_ref.shape, F32)

    def rows(start, d):
        return pl.ds(start, B) if d == 1 else pl.ds(start, B, stride=d)

    for br, (_, d) in enumerate(DILATED_BRANCHES):
        nsub = T // (d * B)

        def step(idx, carry, br=br, d=d, nsub=nsub):
            r = idx // nsub
            i = idx - r * nsub
            qs = r + d * B * i
            ps = r + d * B * jnp.maximum(i - 1, 0)
            var = jnp.where(i == 0, 1, 0)
            qrows = rows(qs, d)
            prows = rows(ps, d)
            q2 = q_ref[qrows, :]
            k2 = jnp.concatenate([k_ref[prows, :], k_ref[qrows, :]], axis=0).astype(BF16)
            v2 = jnp.concatenate([v_ref[prows, :], v_ref[qrows, :]], axis=0).astype(BF16)
            zero = jnp.zeros_like(q2)
            s_a = lax.dot_general(jnp.where(head_a, q2, zero).astype(BF16), k2, (((1,), (1,)), ((), ())),
                                  preferred_element_type=F32) + bm_ref[br, var, 0]
            s_b = lax.dot_general(jnp.where(head_a, zero, q2).astype(BF16), k2, (((1,), (1,)), ((), ())),
                                  preferred_element_type=F32) + bm_ref[br, var, 1]
            m_old = m_ref[qrows, :]
            m_blk = jnp.where(head_a, jnp.max(s_a, axis=1, keepdims=True), jnp.max(s_b, axis=1, keepdims=True))
            m_new = jnp.maximum(m_old, m_blk)
            alpha = jnp.exp(m_old - m_new)
            p_a = jnp.exp(s_a - m_new[:, 0:1])
            p_b = jnp.exp(s_b - m_new[:, HEAD_DIM:HEAD_DIM + 1])
            l_blk = jnp.where(head_a, jnp.sum(p_a, axis=1, keepdims=True), jnp.sum(p_b, axis=1, keepdims=True))
            pv = jnp.where(head_a, jnp.dot(p_a.astype(BF16), v2, preferred_element_type=F32),
                           jnp.dot(p_b.astype(BF16), v2, preferred_element_type=F32))
            m_ref[qrows, :] = m_new
            l_ref[qrows, :] = alpha * l_ref[qrows, :] + l_blk
            o_ref[qrows, :] = alpha * o_ref[qrows, :] + pv
            return carry

        lax.fori_loop(0, d * nsub, step, 0)

    o_ref[...] = o_ref[...] / l_ref[...]


def _attn_prompt(qkv, bm, *, T, nseq):
    hp = D_MODEL // V7X_LANES
    col = lambda which: (lambda b, h: (0, b * 3 * hp + which * hp + h))
    return pl.pallas_call(
        functools.partial(_attn_prompt_kernel, T=T),
        grid=(nseq, hp),
        in_specs=[pl.BlockSpec((T, V7X_LANES), col(0)), pl.BlockSpec((T, V7X_LANES), col(1)),
                  pl.BlockSpec((T, V7X_LANES), col(2)),
                  pl.BlockSpec((3, 2, 2, ATT_BLOCK, 2 * ATT_BLOCK), lambda b, h: (0, 0, h, 0, 0))],
        out_specs=pl.BlockSpec((T, V7X_LANES), lambda b, h: (0, b * hp + h)),
        out_shape=jax.ShapeDtypeStruct((T, nseq * D_MODEL), F32),
        scratch_shapes=[pltpu.VMEM((T, V7X_LANES), F32), pltpu.VMEM((T, V7X_LANES), F32)],
        compiler_params=_cparams(("parallel", "parallel"), 48),
        name="attn_prompt",
    )(qkv, qkv, qkv, bm)


SAMP_TAIL = 512
SAMP_CHUNK_ROWS = 96
SAMP_NCHUNK = 8
SAMP_NEW0 = SAMP_TAIL + SAMP_NCHUNK * SAMP_CHUNK_ROWS
SAMP_NCOL = SAMP_NEW0 + V7X_LANES


def _sample_bias_tables(rel_bias, past, tq):
    pos = np.full((SAMP_NCOL,), -10 ** 6, np.int64)
    pos[:SAMP_TAIL] = past - SAMP_TAIL + np.arange(SAMP_TAIL)
    for c in range(SAMP_NCHUNK):
        pos[SAMP_TAIL + c * SAMP_CHUNK_ROWS:SAMP_TAIL + (c + 1) * SAMP_CHUNK_ROWS] = 16 * np.arange(SAMP_CHUNK_ROWS) + c
    pos[SAMP_NEW0:SAMP_NEW0 + tq] = past + np.arange(tq)
    qpos = past + np.arange(tq)
    dist = qpos[:, None] - pos[None, :]
    tabs = []
    for window, dil in DILATED_BRANCHES:
        valid = (dist >= 0) & (dist <= window) & (dist % dil == 0) & (pos[None, :] >= 0)
        bucket = _t5_bucket(np.clip(dist, 0, window))
        bias = jnp.transpose(rel_bias.astype(F32)[bucket], (2, 0, 1))
        tabs.append(jnp.where(valid[None], bias, NEG_BIG).reshape(N_HEADS * tq, SAMP_NCOL))
    return jnp.stack(tabs)


def _attn_sample_kernel(qkv_ref, kt_ref, kc_ref, vt_ref, vc_ref, bias_ref, o_ref, kall, vall, *, tq):
    qkv = qkv_ref[...]
    q = qkv[:, :D_MODEL]
    nrow = N_HEADS * tq
    pad = jnp.zeros((V7X_LANES - tq, D_MODEL), F32)
    for src_t, src_c, new, dst in ((kt_ref, kc_ref, qkv[:, D_MODEL:2 * D_MODEL], kall),
                                   (vt_ref, vc_ref, qkv[:, 2 * D_MODEL:], vall)):
        dst[0:SAMP_TAIL, :] = src_t[...].astype(BF16)
        for c in range(SAMP_NCHUNK):
            r0 = SAMP_TAIL + c * SAMP_CHUNK_ROWS
            dst[r0:r0 + SAMP_CHUNK_ROWS, :] = src_c[:, c * D_MODEL:(c + 1) * D_MODEL].astype(BF16)
        dst[SAMP_NEW0:SAMP_NCOL, :] = jnp.concatenate([new, pad], axis=0).astype(BF16)
    row_head = lax.broadcasted_iota(jnp.int32, (nrow, D_MODEL), 0) // tq
    lane_head = lax.broadcasted_iota(jnp.int32, (nrow, D_MODEL), 1) // HEAD_DIM
    own = row_head == lane_head
    qbd = jnp.where(own, jnp.concatenate([q] * N_HEADS, axis=0), 0.0).astype(BF16)
    s = lax.dot_general(qbd, kall[...], (((1,), (1,)), ((), ())), preferred_element_type=F32)
    ms, es, ss = [], [], []
    for br in range(3):
        lg = s + bias_ref[br]
        m = jnp.max(lg, axis=1, keepdims=True)
        e = jnp.exp(lg - m)
        ms.append(m)
        es.append(e)
        ss.append(jnp.sum(e, axis=1, keepdims=True))
    m_all = jnp.maximum(jnp.maximum(ms[0], ms[1]), ms[2])
    sc = [jnp.exp(m - m_all) for m in ms]
    den = sc[0] * ss[0] + sc[1] * ss[1] + sc[2] * ss[2]
    p = sc[0] * es[0] + sc[1] * es[1] + sc[2] * es[2]
    o = jnp.dot(p.astype(BF16), vall[...], preferred_element_type=F32) / den
    o = jnp.where(own, o, 0.0)
    out = o[0:tq, :]
    for h in range(1, N_HEADS):
        out = out + o[h * tq:(h + 1) * tq, :]
    o_ref[...] = out


def _attn_sample(qkv_b, cache_k, cache_v, bias):
    nb, tq, _ = qkv_b.shape
    win = cache_k.shape[1]
    assert win == 2048 and tq == V7X_SUBLANES
    kt = cache_k.reshape(nb, win, D_MODEL)
    vt = cache_v.reshape(nb, win, D_MODEL)
    kc = cache_k.reshape(nb, win // 16, 16 * D_MODEL)
    vc = cache_v.reshape(nb, win // 16, 16 * D_MODEL)
    tail_spec = pl.BlockSpec((None, SAMP_TAIL, D_MODEL), lambda b: (b, win // SAMP_TAIL - 1, 0))
    chunk_spec = pl.BlockSpec((None, SAMP_CHUNK_ROWS, SAMP_NCHUNK * D_MODEL), lambda b: (b, 0, 0))
    return pl.pallas_call(
        functools.partial(_attn_sample_kernel, tq=tq),
        grid=(nb,),
        in_specs=[pl.BlockSpec((None, tq, 3 * D_MODEL), lambda b: (b, 0, 0)),
                  tail_spec, chunk_spec, tail_spec, chunk_spec,
                  pl.BlockSpec(bias.shape, lambda b: (0, 0, 0))],
        out_specs=pl.BlockSpec((None, tq, D_MODEL), lambda b: (b, 0, 0)),
        out_shape=jax.ShapeDtypeStruct((nb, tq, D_MODEL), F32),
        scratch_shapes=[pltpu.VMEM((SAMP_NCOL, D_MODEL), BF16), pltpu.VMEM((SAMP_NCOL, D_MODEL), BF16)],
        compiler_params=_cparams(("parallel",), 56),
        name="attn_sample",
    )(qkv_b, kt, kc, vt, vc, bias)


FF_CHUNK = 512
FF_NCHUNK = D_FF // FF_CHUNK


def _post_ffn(x1, f, p, wg, wp, g, b):
    x2 = _layer_norm(DN_ALPHA * x1 + f, g, b)
    return x2 + _sigmoid(_bdot(x2, wg)) * _bdot(p, wp)


def _ffn_kernel(x_ref, p_ref, wa_ref, wb_ref, wo_ref, wg_ref, wp_ref, g_ref, b_ref, o_ref, acc_ref):
    j = pl.program_id(1)
    xb = x_ref[...].astype(BF16)
    a = jnp.dot(xb, wa_ref[...].astype(BF16), preferred_element_type=F32)
    c = jnp.dot(xb, wb_ref[...].astype(BF16), preferred_element_type=F32)
    part = _bdot(a * _sigmoid(a) * c, wo_ref[...])

    @pl.when(j == 0)
    def _():
        acc_ref[...] = part

    @pl.when(j > 0)
    def _():
        acc_ref[...] += part

    @pl.when(j == FF_NCHUNK - 1)
    def _():
        o_ref[...] = _post_ffn(x_ref[...], acc_ref[...], p_ref[...], wg_ref[...], wp_ref[...], g_ref[...], b_ref[...])


def _ffn_dense(x, p, w_in, w_out, layer, wg, wp, g, b, *, TM):
    n = x.shape[0]
    row = lambda i, j: (i, 0)
    const = lambda i, j: (0, 0)
    return pl.pallas_call(
        _ffn_kernel,
        grid=(n // TM, FF_NCHUNK),
        in_specs=[pl.BlockSpec((TM, D_MODEL), row), pl.BlockSpec((TM, PLE_DIM), row),
                  pl.BlockSpec((None, D_MODEL, FF_CHUNK), lambda i, j: (layer, 0, j)),
                  pl.BlockSpec((None, D_MODEL, FF_CHUNK), lambda i, j: (layer, 0, FF_NCHUNK + j)),
                  pl.BlockSpec((None, FF_CHUNK, D_MODEL), lambda i, j: (layer, j, 0)),
                  pl.BlockSpec(wg.shape, const), pl.BlockSpec(wp.shape, const),
                  pl.BlockSpec((1, D_MODEL), const), pl.BlockSpec((1, D_MODEL), const)],
        out_specs=pl.BlockSpec((TM, D_MODEL), row),
        out_shape=jax.ShapeDtypeStruct((n, D_MODEL), F32),
        scratch_shapes=[pltpu.VMEM((TM, D_MODEL), F32)],
        compiler_params=_cparams(("parallel", "arbitrary"), 56),
        name="ffn_dense",
    )(x, p, w_in, w_in, w_out, wg, wp, g, b)


def _router_kernel(x_ref, w_ref, o_ref):
    logits = jnp.dot(x_ref[...], w_ref[...], preferred_element_type=F32, precision=lax.Precision.HIGHEST)
    lane = lax.broadcasted_iota(jnp.int32, logits.shape, 1)
    logits = jnp.where(lane < N_EXPERTS, logits, -jnp.inf)
    m1 = jnp.max(logits, axis=1, keepdims=True)
    i1 = jnp.min(jnp.where(logits == m1, lane, V7X_LANES), axis=1, keepdims=True)
    rest = jnp.where(lane == i1, -jnp.inf, logits)
    m2 = jnp.max(rest, axis=1, keepdims=True)
    i2 = jnp.min(jnp.where(rest == m2, lane, V7X_LANES), axis=1, keepdims=True)
    z = jnp.exp(m2 - m1)
    g1 = 1.0 / (1.0 + z)
    g2 = z / (1.0 + z)
    out = jnp.where(lane == 0, i1.astype(F32), jnp.where(lane == 1, i2.astype(F32),
                                                         jnp.where(lane == 2, g1, jnp.where(lane == 3, g2, 0.0))))
    o_ref[...] = out


def _router(x, w_pad, *, TM):
    n = x.shape[0]
    return pl.pallas_call(
        _router_kernel,
        grid=(n // TM,),
        in_specs=[pl.BlockSpec((TM, D_MODEL), lambda i: (i, 0)), pl.BlockSpec(w_pad.shape, lambda i: (0, 0))],
        out_specs=pl.BlockSpec((TM, V7X_LANES), lambda i: (i, 0)),
        out_shape=jax.ShapeDtypeStruct((n, V7X_LANES), F32),
        compiler_params=_cparams(("parallel",), 32),
        name="moe_router",
    )(x, w_pad)


MOE_TM = 512
MOE_TD = 256


def _dispatch_kernel(dest_ref, x_ref, xs_in_ref, xs_ref, sem):
    del xs_in_ref

    def copy(r, k):
        return pltpu.make_async_copy(x_ref.at[pl.ds(r, 1), :], xs_ref.at[pl.ds(dest_ref[0, 0, 2 * r + k], 1), :], sem)

    def start(r, c):
        copy(r, 0).start()
        copy(r, 1).start()
        return c

    def wait(r, c):
        copy(r, 0).wait()
        copy(r, 1).wait()
        return c

    lax.fori_loop(0, MOE_TD, start, 0)
    lax.fori_loop(0, MOE_TD, wait, 0)


def _dispatch(x, dest, xs):
    n = x.shape[0]
    nt = n // MOE_TD
    return pl.pallas_call(
        _dispatch_kernel,
        grid=(nt,),
        in_specs=[pl.BlockSpec((1, 1, 2 * MOE_TD), lambda i: (i, 0, 0), memory_space=pltpu.SMEM),
                  pl.BlockSpec((MOE_TD, D_MODEL), lambda i: (i, 0)),
                  pl.BlockSpec(memory_space=pl.ANY)],
        out_specs=pl.BlockSpec(memory_space=pl.ANY),
        out_shape=jax.ShapeDtypeStruct(xs.shape, xs.dtype),
        scratch_shapes=[pltpu.SemaphoreType.DMA(())],
        input_output_aliases={2: 0},
        compiler_params=_cparams(("arbitrary",), 32),
        name="moe_dispatch",
    )(dest.reshape(nt, 1, 2 * MOE_TD), x, xs)


def _expert_kernel(te_ref, nu_ref, x_ref, wa_ref, wb_ref, wo_ref, o_ref, acc_ref):
    i = pl.program_id(0)
    j = pl.program_id(1)

    @pl.when(i < nu_ref[0])
    def _():
        xb = x_ref[...].astype(BF16)
        a = jnp.dot(xb, wa_ref[...].astype(BF16), preferred_element_type=F32)
        c = jnp.dot(xb, wb_ref[...].astype(BF16), preferred_element_type=F32)
        part = _bdot(a * _sigmoid(a) * c, wo_ref[...])

        @pl.when(j == 0)
        def _():
            acc_ref[...] = part

        @pl.when(j > 0)
        def _():
            acc_ref[...] += part

    @pl.when((j == FF_NCHUNK - 1) & (i < nu_ref[0]))
    def _():
        o_ref[...] = acc_ref[...]

    @pl.when((j == FF_NCHUNK - 1) & (i >= nu_ref[0]))
    def _():
        o_ref[...] = jnp.zeros(o_ref.shape, F32)


def _experts(xs, tile_e, n_used, w_in, w_out, layer):
    cap = xs.shape[0]
    nt = cap // MOE_TM
    row = lambda i, j, te, nu: (i, 0)
    return pl.pallas_call(
        _expert_kernel,
        grid_spec=pltpu.PrefetchScalarGridSpec(
            num_scalar_prefetch=2,
            grid=(nt, FF_NCHUNK),
            in_specs=[pl.BlockSpec((MOE_TM, D_MODEL), row),
                      pl.BlockSpec((None, None, D_MODEL, FF_CHUNK), lambda i, j, te, nu: (layer, te[i], 0, j)),
                      pl.BlockSpec((None, None, D_MODEL, FF_CHUNK),
                                   lambda i, j, te, nu: (layer, te[i], 0, FF_NCHUNK + j)),
                      pl.BlockSpec((None, None, FF_CHUNK, D_MODEL), lambda i, j, te, nu: (layer, te[i], j, 0))],
            out_specs=pl.BlockSpec((MOE_TM, D_MODEL), row),
            scratch_shapes=[pltpu.VMEM((MOE_TM, D_MODEL), F32)]),
        out_shape=jax.ShapeDtypeStruct((cap, D_MODEL), F32),
        compiler_params=_cparams(("arbitrary", "arbitrary"), 56),
        name="moe_experts",
    )(tile_e, n_used, xs, w_in, w_in, w_out)


def _combine_kernel(dest_ref, x_ref, p_ref, rt_ref, ys_ref, wg_ref, wp_ref, g_ref, b_ref, o_ref, ybuf, sem):
    def copy(r, k):
        return pltpu.make_async_copy(ys_ref.at[pl.ds(dest_ref[0, 0, 2 * r + k], 1), :], ybuf.at[k, pl.ds(r, 1), :], sem)

    def start(r, c):
        copy(r, 0).start()
        copy(r, 1).start()
        return c

    def wait(r, c):
        copy(r, 0).wait()
        copy(r, 1).wait()
        return c

    lax.fori_loop(0, MOE_TD, start, 0)
    lax.fori_loop(0, MOE_TD, wait, 0)
    rt = rt_ref[...]
    f = rt[:, 2:3] * ybuf[0] + rt[:, 3:4] * ybuf[1]
    o_ref[...] = _post_ffn(x_ref[...], f, p_ref[...], wg_ref[...], wp_ref[...], g_ref[...], b_ref[...])


def _combine(x, p, route, dest, ys, wg, wp, g, b):
    n = x.shape[0]
    nt = n // MOE_TD
    row = lambda i: (i, 0)
    const = lambda i: (0, 0)
    return pl.pallas_call(
        _combine_kernel,
        grid=(nt,),
        in_specs=[pl.BlockSpec((1, 1, 2 * MOE_TD), lambda i: (i, 0, 0), memory_space=pltpu.SMEM),
                  pl.BlockSpec((MOE_TD, D_MODEL), row), pl.BlockSpec((MOE_TD, PLE_DIM), row),
                  pl.BlockSpec((MOE_TD, V7X_LANES), row),
                  pl.BlockSpec(memory_space=pl.ANY),
                  pl.BlockSpec(wg.shape, const), pl.BlockSpec(wp.shape, const),
                  pl.BlockSpec((1, D_MODEL), const), pl.BlockSpec((1, D_MODEL), const)],
        out_specs=pl.BlockSpec((MOE_TD, D_MODEL), row),
        out_shape=jax.ShapeDtypeStruct((n, D_MODEL), F32),
        scratch_shapes=[pltpu.VMEM((2, MOE_TD, D_MODEL), F32), pltpu.SemaphoreType.DMA(())],
        compiler_params=_cparams(("arbitrary",), 32),
        name="moe_combine",
    )(dest.reshape(nt, 1, 2 * MOE_TD), x, p, route, ys, wg, wp, g, b)


def _moe_plan(top_e, n_tiles):
    flat_e = top_e.reshape(-1)
    onehot = (flat_e[:, None] == jnp.arange(N_EXPERTS, dtype=jnp.int32)[None, :]).astype(jnp.int32)
    csum = jnp.cumsum(onehot, axis=0)
    rank = jnp.sum((csum - onehot) * onehot, axis=1)
    counts = csum[-1]
    padded = (counts + MOE_TM - 1) // MOE_TM * MOE_TM
    pad_end = jnp.cumsum(padded)
    pad_start = pad_end - padded
    dest = (pad_start[flat_e] + rank).astype(jnp.int32)
    starts = MOE_TM * jnp.arange(n_tiles, dtype=jnp.int32)
    tile_e = jnp.minimum(jnp.searchsorted(pad_end, starts, side='right'), N_EXPERTS - 1).astype(jnp.int32)
    n_used = (pad_end[-1] // MOE_TM).astype(jnp.int32).reshape(1)
    return dest, tile_e, n_used


def _moe_layer(xs_list, ps_list, w_router, w_in, w_out, layer, wg, wp, g, b):
    w_pad = jnp.zeros((D_MODEL, V7X_LANES), F32).at[:, :N_EXPERTS].set(w_router.astype(F32))
    routes = [_router(x, w_pad, TM=512) for x in xs_list]
    top_e = jnp.concatenate([r[:, :2].astype(jnp.int32) for r in routes], axis=0)
    n_tot = top_e.shape[0]
    n_tiles = -(-(2 * n_tot) // MOE_TM) + N_EXPERTS
    dest, tile_e, n_used = _moe_plan(top_e, n_tiles)
    slots = jnp.zeros((n_tiles * MOE_TM, D_MODEL), F32)
    off = 0
    dests = []
    for x in xs_list:
        d = dest[2 * off:2 * (off + x.shape[0])]
        dests.append(d)
        slots = _dispatch(x, d, slots)
        off += x.shape[0]
    ys = _experts(slots, tile_e, n_used, w_in, w_out, layer)
    return [_combine(x, p, r, d, ys, wg, wp, g, b) for x, p, r, d in zip(xs_list, ps_list, routes, dests)]


def _tm(a):
    b, t, c = a.shape
    return jnp.transpose(a, (1, 0, 2)).reshape(t * b, c)


def _bm(a, b):
    n, c = a.shape
    return jnp.transpose(a.reshape(n // b, b, c), (1, 0, 2))


def kernel(x_prompt, x_sample, state_conv_a, state_s5_re, state_s5_im, cache_k_win, cache_v_win, state_conv_d,
           p_prompt, p_sample, a_w_in, a_conv, a_w_out, s5_lam_re, s5_lam_im, s5_log_dt, s5_b_re, s5_b_im,
           s5_c_re, s5_c_im, s5_d, s5_w_glu, c_w_qkv, c_w_o, rel_bias, d_w_pw1, d_dw, d_ln_g, d_ln_b, d_w_pw2,
           ffn_w_in, ffn_w_out, moe_router, moe_w_in, moe_w_out, ln_mix_g, ln_mix_b, ln_ffn_g, ln_ffn_b,
           ple_w_proj, ple_w_gate):
    bp, tp, _ = x_prompt.shape
    bs, ts, _ = x_sample.shape
    past = cache_k_win.shape[1]
    row = lambda v: v.reshape(1, -1).astype(F32)

    xp = _tm(x_prompt)
    xs = _tm(x_sample)
    pp = [_tm(p_prompt[i]) for i in range(DEPTH)]
    ps = [_tm(p_sample[i]) for i in range(DEPTH)]
    wg = [ple_w_gate[i].astype(BF16) for i in range(DEPTH)]
    wp = [ple_w_proj[i].astype(BF16) for i in range(DEPTH)]

    a_in, a_out = a_w_in.astype(BF16), a_w_out.astype(BF16)
    a_cv = a_conv.astype(F32)
    g0, b0 = row(ln_mix_g[0]), row(ln_mix_b[0])
    xp, ca_p = _conv_mixer(xp, jnp.zeros(((CONV_A_W - 1) * bp, D_MODEL), F32), a_in, a_cv, a_out, g0, b0, None,
                           mode='a', R=bp, TB=512)
    seq_r = 32
    xs3, ca_s = _conv_mixer(xs.reshape(ts, bs, D_MODEL), jnp.transpose(state_conv_a.astype(F32), (1, 0, 2)),
                            a_in, a_cv, a_out, g0, b0, None, mode='a', R=seq_r, TB=ts * seq_r)
    xs = xs3.reshape(ts * bs, D_MODEL)
    f0 = (row(ln_ffn_g[0]), row(ln_ffn_b[0]))
    xp = _ffn_dense(xp, pp[0], ffn_w_in, ffn_w_out, 0, wg[0], wp[0], *f0, TM=1024)
    xs = _ffn_dense(xs, ps[0], ffn_w_in, ffn_w_out, 0, wg[0], wp[0], *f0, TM=1024)

    ar, ai, bbr, bbi = _s5_discretize(s5_lam_re.astype(F32), s5_lam_im.astype(F32), s5_log_dt.astype(F32),
                                      s5_b_re.astype(F32), s5_b_im.astype(F32))
    w_b, wc_re, wc_im = _s5_block_weights(bbr, bbi, s5_c_re.astype(F32), s5_c_im.astype(F32))
    a8 = jnp.broadcast_to(jnp.concatenate([ar.reshape(1, -1), ai.reshape(1, -1)], axis=1), (V7X_SUBLANES, 2 * S5_CH))
    dsk = row(s5_d)
    wglu = s5_w_glu.astype(BF16)
    g1, b1 = row(ln_mix_g[1]), row(ln_mix_b[1])
    xp, st_p = _s5_mixer(xp, jnp.zeros((V7X_SUBLANES, 2 * S5_CH), F32), a8, w_b, wc_re, wc_im, dsk, wglu, g1, b1,
                         R=bp, TB=256, nchunk=1)
    h0_s = jnp.concatenate([state_s5_re.reshape(bs, S5_CH), state_s5_im.reshape(bs, S5_CH)], axis=1).astype(F32)
    xs3, st_s = _s5_mixer(xs.reshape(ts, bs, D_MODEL), h0_s, a8, w_b, wc_re, wc_im, dsk, wglu, g1, b1,
                          R=seq_r, TB=ts * seq_r, nchunk=bs // seq_r)
    xs = xs3.reshape(ts * bs, D_MODEL)
    f1 = (row(ln_ffn_g[1]), row(ln_ffn_b[1]))
    xp, xs = _moe_layer([xp, xs], [pp[1], ps[1]], moe_router[0], moe_w_in, moe_w_out, 0, wg[1], wp[1], *f1)

    w_qkv, w_o = c_w_qkv.astype(BF16), c_w_o.astype(BF16)
    g2, b2 = row(ln_mix_g[2]), row(ln_mix_b[2])
    qkv_p = _qkv_proj(xp, w_qkv, TM=512)
    qkv_s = _qkv_proj(xs, w_qkv, TM=512)
    o_p = _attn_prompt(qkv_p.reshape(tp, bp * 3 * D_MODEL), _prompt_bias_tables(rel_bias), T=tp, nseq=bp)
    qkv_sb = _bm(qkv_s, bs)
    o_s = _attn_sample(qkv_sb, cache_k_win.astype(F32), cache_v_win.astype(F32),
                       _sample_bias_tables(rel_bias, past, ts))
    xp = _proj_ln(o_p.reshape(tp * bp, D_MODEL), xp, w_o, g2, b2, TM=512)
    xs = _proj_ln(_tm(o_s), xs, w_o, g2, b2, TM=512)
    f2 = (row(ln_ffn_g[2]), row(ln_ffn_b[2]))
    xp = _ffn_dense(xp, pp[2], ffn_w_in, ffn_w_out, 1, wg[2], wp[2], *f2, TM=1024)
    xs = _ffn_dense(xs, ps[2], ffn_w_in, ffn_w_out, 1, wg[2], wp[2], *f2, TM=1024)

    d_in, d_out = d_w_pw1.astype(BF16), d_w_pw2.astype(BF16)
    d_cv = d_dw.astype(F32)
    dln = (row(d_ln_g), row(d_ln_b))
    g3, b3 = row(ln_mix_g[3]), row(ln_mix_b[3])
    xp, cd_p = _conv_mixer(xp, jnp.zeros(((CONV_D_W - 1) * bp, D_MODEL), F32), d_in, d_cv, d_out, g3, b3, dln,
                           mode='d', R=bp, TB=512)
    xs3, cd_s = _conv_mixer(xs.reshape(ts, bs, D_MODEL), jnp.transpose(state_conv_d.astype(F32), (1, 0, 2)),
                            d_in, d_cv, d_out, g3, b3, dln, mode='d', R=seq_r, TB=ts * seq_r)
    xs = xs3.reshape(ts * bs, D_MODEL)
    f3 = (row(ln_ffn_g[3]), row(ln_ffn_b[3]))
    xp, xs = _moe_layer([xp, xs], [pp[3], ps[3]], moe_router[1], moe_w_in, moe_w_out, 1, wg[3], wp[3], *f3)

    keep = min(2048, tp)
    kv_p = qkv_p.reshape(tp, bp, 3, N_HEADS, HEAD_DIM)[tp - keep:]
    k_p = jnp.transpose(kv_p[:, :, 1], (1, 0, 2, 3))
    v_p = jnp.transpose(kv_p[:, :, 2], (1, 0, 2, 3))
    k_s = qkv_sb[:, :, D_MODEL:2 * D_MODEL].reshape(bs, ts, N_HEADS, HEAD_DIM)
    v_s = qkv_sb[:, :, 2 * D_MODEL:].reshape(bs, ts, N_HEADS, HEAD_DIM)
    s5p = st_p[4:4 + bp]
    return (_bm(xp, bp), _bm(xs, bs), _bm(ca_p, bp), jnp.transpose(ca_s, (1, 0, 2)),
            s5p[:, :S5_CH].reshape(bp, S5_GROUPS, S5_STATE), s5p[:, S5_CH:].reshape(bp, S5_GROUPS, S5_STATE),
            st_s[:, :S5_CH].reshape(bs, S5_GROUPS, S5_STATE), st_s[:, S5_CH:].reshape(bs, S5_GROUPS, S5_STATE),
            k_p, v_p, k_s, v_s, _bm(cd_p, bp), jnp.transpose(cd_s, (1, 0, 2)))
```
